```python
import jax, jax.numpy as jnp
from jax import lax
import numpy as np

D_MODEL = 1024
BATCH = 8
SEQ = 2048
DEPTH = 4

HEAD_DIM = 64
A_HEADS = 8
A_KV_HEADS = 2
A_GROUP = A_HEADS // A_KV_HEADS
A_WINDOW = 128
A_BLOCK = 128
A_BAND_BLOCKS = 3
B_HEADS = 8
GRID_W = 64
B_MAX_ROWS = 8
B_WIN_C = 16
B_QBLOCK_C = 16
B_KBAND_C = 32
FFN_HIDDEN = -(-8 * D_MODEL // (3 * 256)) * 256
ROPE_THETA = 10000.0
EPS = 1e-6
NEG = -1e30

A_Q_W = A_HEADS * HEAD_DIM
A_KV_W = A_KV_HEADS * HEAD_DIM
B_W = B_HEADS * HEAD_DIM
IN_WIDTHS = [A_Q_W, A_KV_W, A_KV_W, B_W, B_W, B_W, D_MODEL, D_MODEL]
IN_COLS = sum(IN_WIDTHS)
IN_SPLITS = np.cumsum(IN_WIDTHS)[:-1].tolist()

kernel_name = "hybrid_window_gqa_neighbourhood_encoder"


def rms_norm(x, g):
    xf = x.astype(jnp.float32)
    y = xf * lax.rsqrt(jnp.mean(xf * xf, axis=-1, keepdims=True) + EPS)
    return (y * g.astype(jnp.float32)).astype(x.dtype)


def rope_tables(positions):
    half = HEAD_DIM // 2
    inv = ROPE_THETA ** (-jnp.arange(half, dtype=jnp.float32) / half)
    ang = positions.astype(jnp.float32)[:, None] * inv[None, :]
    return jnp.cos(ang)[None, :, None, :], jnp.sin(ang)[None, :, None, :]


def apply_rope(x, cos, sin):
    half = HEAD_DIM // 2
    xf = x.astype(jnp.float32)
    x1, x2 = xf[..., :half], xf[..., half:]
    out = jnp.concatenate([x1 * cos - x2 * sin, x2 * cos + x1 * sin], axis=-1)
    return out.astype(x.dtype)


def window_gqa(q, k, v, sink):
    b, s = q.shape[:2]
    nb = s // A_BLOCK
    pad = ((0, 0), (A_BLOCK, A_BLOCK), (0, 0), (0, 0))
    kp = jnp.pad(k, pad).reshape(b, nb + 2, A_BLOCK, A_KV_HEADS, HEAD_DIM)
    vp = jnp.pad(v, pad).reshape(b, nb + 2, A_BLOCK, A_KV_HEADS, HEAD_DIM)
    kb = jnp.concatenate([kp[:, o:o + nb] for o in range(A_BAND_BLOCKS)], axis=2)
    vb = jnp.concatenate([vp[:, o:o + nb] for o in range(A_BAND_BLOCKS)], axis=2)
    qb = q.reshape(b, nb, A_BLOCK, A_KV_HEADS, A_GROUP, HEAD_DIM)
    scores = jnp.einsum('bnqkgd,bnjkd->bnkgqj', qb, kb).astype(jnp.float32) * (HEAD_DIM ** -0.5)
    n_i = np.arange(nb)[:, None, None]
    r_i = np.arange(A_BLOCK)[None, :, None]
    j_i = np.arange(A_BAND_BLOCKS * A_BLOCK)[None, None, :]
    kpos = n_i * A_BLOCK - A_BLOCK + j_i
    qpos = n_i * A_BLOCK + r_i
    valid = (np.abs(kpos - qpos) <= A_WINDOW) & (kpos >= 0) & (kpos < s)
    scores = jnp.where(valid[None, :, None, None], scores, NEG)
    sink_col = jnp.broadcast_to(
        sink.astype(jnp.float32).reshape(A_KV_HEADS, A_GROUP)[None, None, :, :, None, None],
        scores.shape[:-1] + (1,))
    p = jax.nn.softmax(jnp.concatenate([scores, sink_col], axis=-1), axis=-1)[..., :-1]
    out = jnp.einsum('bnkgqj,bnjkd->bnqkgd', p.astype(v.dtype), vb)
    return out.reshape(b, s, A_Q_W)


def neighbourhood_attn(q, k, v, rel_bias):
    b, s = q.shape[:2]
    rows = s // GRID_W
    wr = min(B_MAX_ROWS, rows)
    ncb = GRID_W // B_QBLOCK_C
    r = np.arange(rows)
    rs = np.clip(r - wr // 2, 0, rows - wr)
    row_idx = rs[:, None] + np.arange(wr)[None, :]
    jb = np.arange(ncb)
    kc0 = np.clip(jb * B_QBLOCK_C - B_WIN_C // 2, 0, GRID_W - B_KBAND_C)
    col_idx = kc0[:, None] + np.arange(B_KBAND_C)[None, :]
    qcol = jb[:, None] * B_QBLOCK_C + np.arange(B_QBLOCK_C)[None, :]
    cs = np.clip(qcol - B_WIN_C // 2, 0, GRID_W - B_WIN_C)
    valid = (col_idx[:, None, :] >= cs[..., None]) & (col_idx[:, None, :] < cs[..., None] + B_WIN_C)
    dr_i = row_idx - r[:, None] + B_MAX_ROWS - 1
    dc_i = np.clip(col_idx[:, None, :] - qcol[..., None] + B_WIN_C - 1, 0, 2 * B_WIN_C - 2)
    bias = rel_bias[:, dr_i[:, None, None, :, None], dc_i[None, :, :, None, :]]
    bias = jnp.where(valid[None, None, :, :, None, :], bias.astype(jnp.float32), NEG)
    kg = k.reshape(b, rows, GRID_W, B_HEADS, HEAD_DIM)
    vg = v.reshape(b, rows, GRID_W, B_HEADS, HEAD_DIM)
    gr = row_idx[:, None, :, None]
    gc = col_idx[None, :, None, :]
    kb = kg[:, gr, gc]
    vb = vg[:, gr, gc]
    qg = q.reshape(b, rows, ncb, B_QBLOCK_C, B_HEADS, HEAD_DIM)
    scores = jnp.einsum('brjqhd,brjwkhd->bhrjqwk', qg, kb).astype(jnp.float32) * (HEAD_DIM ** -0.5)
    scores = scores + bias[None]
    shp = scores.shape
    p = jax.nn.softmax(scores.reshape(shp[:-2] + (wr * B_KBAND_C,)), axis=-1).reshape(shp)
    out = jnp.einsum('bhrjqwk,brjwkhd->brjqhd', p.astype(v.dtype), vb)
    return out.reshape(b, s, B_W)


def setup_inputs(seed: int = 0) -> dict:
    key = jax.random.key(seed)
    ks = jax.random.split(key, 16)
    f32 = jnp.float32

    def w(k, shape, fan_in):
        return jax.random.normal(k, shape, f32) * (fan_in ** -0.5)

    def gain(k, shape):
        return 1.0 + 0.02 * jax.random.normal(k, shape, f32)

    return {
        "x": jax.random.normal(ks[0], (BATCH, SEQ, D_MODEL), f32),
        "positions": jnp.arange(SEQ, dtype=jnp.int32),
        "attn_norm_g": gain(ks[1], (DEPTH, D_MODEL)),
        "w_in": w(ks[2], (DEPTH, D_MODEL, IN_COLS), D_MODEL),
        "q_norm_a": gain(ks[3], (DEPTH, HEAD_DIM)),
        "k_norm_a": gain(ks[4], (DEPTH, HEAD_DIM)),
        "sink_a": 0.5 * jax.random.normal(ks[5], (DEPTH, A_HEADS), f32),
        "q_norm_b": gain(ks[6], (DEPTH, HEAD_DIM)),
        "k_norm_b": gain(ks[7], (DEPTH, HEAD_DIM)),
        "rel_bias_b": 0.1 * jax.random.normal(ks[8], (DEPTH, B_HEADS, 2 * B_MAX_ROWS - 1, 2 * B_WIN_C - 1), f32),
        "w_proj_a": w(ks[9], (DEPTH, A_Q_W, D_MODEL), A_Q_W),
        "w_proj_b": w(ks[10], (DEPTH, B_W, D_MODEL), B_W),
        "w_out": w(ks[11], (DEPTH, D_MODEL, D_MODEL), D_MODEL),
        "ffn_norm_g": gain(ks[12], (DEPTH, D_MODEL)),
        "w_gate": w(ks[13], (DEPTH, D_MODEL, FFN_HIDDEN), D_MODEL),
        "w_up": w(ks[14], (DEPTH, D_MODEL, FFN_HIDDEN), D_MODEL),
        "w_down": w(ks[15], (DEPTH, FFN_HIDDEN, D_MODEL), FFN_HIDDEN),
    }


def reference(x, positions, attn_norm_g, w_in, q_norm_a, k_norm_a, sink_a, q_norm_b, k_norm_b,
              rel_bias_b, w_proj_a, w_proj_b, w_out, ffn_norm_g, w_gate, w_up, w_down):
    b, s, _ = x.shape
    cos, sin = rope_tables(positions)
    for l in range(DEPTH):
        h = rms_norm(x, attn_norm_g[l])
        z = jnp.einsum('bsd,dc->bsc', h, w_in[l])
        qa, ka, va, qb, kb, vb, ga, gb = jnp.split(z, IN_SPLITS, axis=-1)
        qa = apply_rope(rms_norm(qa.reshape(b, s, A_HEADS, HEAD_DIM), q_norm_a[l]), cos, sin)
        ka = apply_rope(rms_norm(ka.reshape(b, s, A_KV_HEADS, HEAD_DIM), k_norm_a[l]), cos, sin)
        va = va.reshape(b, s, A_KV_HEADS, HEAD_DIM)
        ya = jnp.einsum('bsc,cd->bsd', window_gqa(qa, ka, va, sink_a[l]), w_proj_a[l])
        qb = rms_norm(qb.reshape(b, s, B_HEADS, HEAD_DIM), q_norm_b[l])
        kb = rms_norm(kb.reshape(b, s, B_HEADS, HEAD_DIM), k_norm_b[l])
        vb = vb.reshape(b, s, B_HEADS, HEAD_DIM)
        yb = jnp.einsum('bsc,cd->bsd', neighbourhood_attn(qb, kb, vb, rel_bias_b[l]), w_proj_b[l])
        mixed = jax.nn.sigmoid(ga) * ya + jax.nn.sigmoid(gb) * yb
        x = x + jnp.einsum('bsd,de->bse', mixed, w_out[l])
        h = rms_norm(x, ffn_norm_g[l])
        u = jax.nn.silu(jnp.einsum('bsd,df->bsf', h, w_gate[l])) * jnp.einsum('bsd,df->bsf', h, w_up[l])
        x = x + jnp.einsum('bsf,fd->bsd', u, w_down[l])
    return x
```

```python
import functools

import numpy as np
import jax
import jax.numpy as jnp
from jax import lax
from jax.experimental import pallas as pl
from jax.experimental.pallas import tpu as pltpu

D_MODEL = 1024
HEAD_DIM = 64
A_HEADS = 8
A_KV_HEADS = 2
A_GROUP = A_HEADS // A_KV_HEADS
A_WINDOW = 128
A_BLOCK = 128
A_BAND = 3 * A_BLOCK
B_HEADS = 8
GRID_W = 64
B_WIN_R = 8
B_WIN_C = 16
FFN_HIDDEN = 2816
ROPE_THETA = 10000.0
EPS = 1e-6
NEG = -1e30

A_Q_W = A_HEADS * HEAD_DIM
A_KV_W = A_KV_HEADS * HEAD_DIM
B_W = B_HEADS * HEAD_DIM
IN_WIDTHS = (A_Q_W, A_KV_W, A_KV_W, B_W, B_W, B_W, D_MODEL, D_MODEL)
IN_OFFS = tuple(int(v) for v in np.cumsum((0,) + IN_WIDTHS))
IN_COLS = IN_OFFS[-1]

LANES = 128
MXU_DIM = 256
VMEM_BYTES_V7X = 64 * 1024 * 1024
HEADS_PER_VREG = LANES // HEAD_DIM

TM_IN = 512
TM_FFN = 512
FFN_CHUNK = 704

F32 = jnp.float32
BF16 = jnp.bfloat16


def _const_spec(shape):
    nd = len(shape)
    return pl.BlockSpec(shape, lambda *_: (0,) * nd, pipeline_mode=pl.Buffered(1))


def _vmem_limit(block_bytes, temp_bytes):
    need = int(block_bytes + temp_bytes)
    assert need < VMEM_BYTES_V7X, need
    return need


def _nbytes(shape, dtype):
    return int(np.prod(shape)) * jnp.dtype(dtype).itemsize


def _in_proj_kernel(x_ref, g_ref, w_ref, cos_ref, sin_ref, gqa_ref, gka_ref, gqb_ref, gkb_ref,
                    bd_ref, qa_ref, ka_ref, va_ref, qb_ref, kb_ref, vb_ref, ga_ref, gb_ref):
    x = x_ref[...]
    h = (x * lax.rsqrt(jnp.mean(x * x, axis=-1, keepdims=True) + EPS) * g_ref[...]).astype(BF16)
    cos = cos_ref[...]
    sin = sin_ref[...]
    lane = lax.broadcasted_iota(jnp.int32, (x.shape[0], LANES), 1)
    first_half = (lane % HEAD_DIM) < (HEAD_DIM // 2)

    def proj(seg, lo, width):
        off = IN_OFFS[seg] + lo
        return jnp.dot(h, w_ref[:, off:off + width], preferred_element_type=F32)

    def head_norm(z, gain):
        zz = z * z
        hi = zz.astype(BF16)
        lo = (zz - hi.astype(F32)).astype(BF16)
        w = z.shape[1]
        bd = bd_ref[:w, :w]
        ss = (jnp.dot(hi, bd, preferred_element_type=F32)
              + jnp.dot(lo, bd, preferred_element_type=F32))
        return z * lax.rsqrt(ss * (1.0 / HEAD_DIM) + EPS) * gain

    def rope(y):
        rot = jnp.where(first_half,
                        pltpu.roll(y, LANES - HEAD_DIM // 2, 1),
                        pltpu.roll(y, HEAD_DIM // 2, 1))
        return y * cos + rot * sin

    def normed(seg, gain_ref, out_ref, with_rope):
        width = IN_WIDTHS[seg]
        step = min(width, MXU_DIM)
        for lo in range(0, width, step):
            y = head_norm(proj(seg, lo, step), gain_ref[:, lo:lo + step])
            for c in range(0, step, LANES):
                piece = y[:, c:c + LANES]
                if with_rope:
                    piece = rope(piece)
                out_ref[:, lo + c:lo + c + LANES] = piece.astype(out_ref.dtype)

    def plain(seg, out_ref):
        width = IN_WIDTHS[seg]
        step = min(width, 2 * MXU_DIM)
        for lo in range(0, width, step):
            out_ref[:, lo:lo + step] = proj(seg, lo, step).astype(out_ref.dtype)

    normed(0, gqa_ref, qa_ref, True)
    normed(1, gka_ref, ka_ref, True)
    plain(2, va_ref)
    normed(3, gqb_ref, qb_ref, False)
    normed(4, gkb_ref, kb_ref, False)
    plain(5, vb_ref)
    plain(6, ga_ref)
    plain(7, gb_ref)


def _in_proj(x, g, w, cos, sin, gqa, gka, gqb, gkb, bd, seq):
    t = x.shape[0]
    tm = TM_IN
    tiles_per_seq = seq // tm
    row = lambda i: (i, 0)
    pos = lambda i: (i % tiles_per_seq, 0)
    out_w = (A_Q_W, A_KV_W, A_KV_W, B_W, B_W, B_W, D_MODEL, D_MODEL)
    out_dt = (BF16,) * 6 + (F32, F32)
    blocks = (2 * _nbytes((tm, D_MODEL), F32) + _nbytes(w.shape, BF16)
              + 4 * _nbytes((tm, LANES), F32)
              + sum(2 * _nbytes((tm, wd), dt) for wd, dt in zip(out_w, out_dt))
              + _nbytes(bd.shape, BF16) + 8 * _nbytes((1, D_MODEL), F32))
    temps = _nbytes((tm, D_MODEL), BF16) + 8 * _nbytes((tm, 2 * MXU_DIM), F32)
    return pl.pallas_call(
        _in_proj_kernel,
        grid=(t // tm,),
        in_specs=[
            pl.BlockSpec((tm, D_MODEL), row),
            _const_spec((1, D_MODEL)),
            _const_spec(w.shape),
            pl.BlockSpec((tm, LANES), pos),
            pl.BlockSpec((tm, LANES), pos),
            _const_spec(gqa.shape), _const_spec(gka.shape),
            _const_spec(gqb.shape), _const_spec(gkb.shape),
            _const_spec(bd.shape),
        ],
        out_specs=[pl.BlockSpec((tm, wd), row) for wd in out_w],
        out_shape=[jax.ShapeDtypeStruct((t, wd), dt) for wd, dt in zip(out_w, out_dt)],
        compiler_params=pltpu.CompilerParams(
            dimension_semantics=("arbitrary",),
            vmem_limit_bytes=_vmem_limit(blocks, temps)),
        name="in_proj",
    )(x, g, w, cos, sin, gqa, gka, gqb, gkb, bd)


def _win_attn_kernel(sink_ref, q_ref, k_ref, v_ref, o_ref, *, seq):
    n = pl.program_id(1)
    start = pl.multiple_of(jnp.clip(n * A_BLOCK - A_BLOCK, 0, seq - A_BAND), A_BLOCK)
    k = k_ref[0, pl.ds(start, A_BAND), :]
    v = v_ref[0, pl.ds(start, A_BAND), :]
    qpos = n * A_BLOCK + lax.broadcasted_iota(jnp.int32, (A_BLOCK, A_BAND), 0)
    kpos = start + lax.broadcasted_iota(jnp.int32, (A_BLOCK, A_BAND), 1)
    valid = jnp.abs(kpos - qpos) <= A_WINDOW
    lane = lax.broadcasted_iota(jnp.int32, (A_BLOCK, LANES), 1)
    lo_half = lane < HEAD_DIM
    for j in range(A_GROUP):
        qg = q_ref[0, :, j * LANES:(j + 1) * LANES]
        outs = []
        for kvh in range(A_KV_HEADS):
            keep = lo_half if kvh == 0 else jnp.logical_not(lo_half)
            qm = jnp.where(keep, qg, jnp.zeros_like(qg))
            s = lax.dot_general(qm, k, (((1,), (1,)), ((), ())), preferred_element_type=F32)
            s = jnp.where(valid, s, NEG)
            sink = sink_ref[HEADS_PER_VREG * j + kvh]
            m = jnp.maximum(jnp.max(s, axis=-1, keepdims=True), sink)
            e = jnp.exp(s - m)
            den = jnp.sum(e, axis=-1, keepdims=True) + jnp.exp(sink - m)
            o = jnp.dot(e.astype(BF16), v, preferred_element_type=F32) / den
            outs.append(o)
        o_ref[0, :, j * LANES:(j + 1) * LANES] = jnp.where(lo_half, outs[0], outs[1]).astype(o_ref.dtype)


def _win_attn(sink, q, k, v):
    b, s, _ = q.shape
    nb = s // A_BLOCK
    blocks = (4 * _nbytes((A_BLOCK, A_Q_W), BF16) + 4 * _nbytes((s, A_KV_W), BF16))
    temps = 16 * _nbytes((A_BLOCK, A_BAND), F32)
    return pl.pallas_call(
        functools.partial(_win_attn_kernel, seq=s),
        grid=(b, nb),
        in_specs=[
            pl.BlockSpec(memory_space=pltpu.SMEM),
            pl.BlockSpec((1, A_BLOCK, A_Q_W), lambda i, n: (i, n, 0)),
            pl.BlockSpec((1, s, A_KV_W), lambda i, n: (i, 0, 0)),
            pl.BlockSpec((1, s, A_KV_W), lambda i, n: (i, 0, 0)),
        ],
        out_specs=pl.BlockSpec((1, A_BLOCK, A_Q_W), lambda i, n: (i, n, 0)),
        out_shape=jax.ShapeDtypeStruct((b, s, A_Q_W), BF16),
        compiler_params=pltpu.CompilerParams(
            dimension_semantics=("arbitrary", "arbitrary"),
            vmem_limit_bytes=_vmem_limit(blocks, temps)),
        name="win_attn",
    )(sink, q, k, v)


def _nbr_attn_kernel(q_ref, k_ref, v_ref, bias_ref, o_ref, *, rows):
    r = pl.program_id(1)
    rs = jnp.clip(r - B_WIN_R // 2, 0, rows - B_WIN_R)
    start = pl.multiple_of(rs * GRID_W, GRID_W)
    dr0 = rs - r + (B_WIN_R - 1)
    nkeys = B_WIN_R * GRID_W
    lane = lax.broadcasted_iota(jnp.int32, (GRID_W, LANES), 1)
    lo_half = lane < HEAD_DIM
    for j in range(B_HEADS // HEADS_PER_VREG):
        kg = k_ref[0, pl.ds(start, nkeys), j * LANES:(j + 1) * LANES]
        vg = v_ref[0, pl.ds(start, nkeys), j * LANES:(j + 1) * LANES]
        qg = q_ref[0, :, j * LANES:(j + 1) * LANES]
        zero = jnp.zeros_like(qg)
        qs = jnp.concatenate([jnp.where(lo_half, qg, zero), jnp.where(lo_half, zero, qg)], axis=0)
        s = lax.dot_general(qs, kg, (((1,), (1,)), ((), ())), preferred_element_type=F32)
        ps = []
        dens = []
        for hh in range(HEADS_PER_VREG):
            h = HEADS_PER_VREG * j + hh
            bias = jnp.concatenate(
                [bias_ref[h, dr0 + 2 * wp] for wp in range(B_WIN_R // 2)], axis=1)
            sh = s[hh * GRID_W:(hh + 1) * GRID_W] + bias
            m = jnp.max(sh, axis=-1, keepdims=True)
            e = jnp.exp(sh - m)
            dens.append(jnp.sum(e, axis=-1, keepdims=True))
            ps.append(e.astype(BF16))
        o = jnp.dot(jnp.concatenate(ps, axis=0), vg, preferred_element_type=F32)
        o = jnp.where(lo_half, o[:GRID_W] / dens[0], o[GRID_W:] / dens[1])
        o_ref[0, :, j * LANES:(j + 1) * LANES] = o.astype(o_ref.dtype)


def _nbr_attn(q, k, v, bias):
    b, s, _ = q.shape
    rows = s // GRID_W
    blocks = (4 * _nbytes((GRID_W, B_W), BF16) + 4 * _nbytes((s, B_W), BF16)
              + _nbytes(bias.shape, F32))
    temps = 16 * _nbytes((2 * GRID_W, B_WIN_R * GRID_W), F32)
    return pl.pallas_call(
        functools.partial(_nbr_attn_kernel, rows=rows),
        grid=(b, rows),
        in_specs=[
            pl.BlockSpec((1, GRID_W, B_W), lambda i, r: (i, r, 0)),
            pl.BlockSpec((1, s, B_W), lambda i, r: (i, 0, 0)),
            pl.BlockSpec((1, s, B_W), lambda i, r: (i, 0, 0)),
            _const_spec(bias.shape),
        ],
        out_specs=pl.BlockSpec((1, GRID_W, B_W), lambda i, r: (i, r, 0)),
        out_shape=jax.ShapeDtypeStruct((b, s, B_W), BF16),
        compiler_params=pltpu.CompilerParams(
            dimension_semantics=("arbitrary", "arbitrary"),
            vmem_limit_bytes=_vmem_limit(blocks, temps)),
        name="nbr_attn",
    )(q, k, v, bias)


def _mix_ffn_kernel(x_ref, oa_ref, ob_ref, ga_ref, gb_ref, wpa_ref, wpb_ref, wout_ref, g_ref,
                    wg_ref, wu_ref, wd_ref, o_ref):
    ya = jnp.dot(oa_ref[...], wpa_ref[...], preferred_element_type=F32)
    yb = jnp.dot(ob_ref[...], wpb_ref[...], preferred_element_type=F32)
    mixed = jax.nn.sigmoid(ga_ref[...]) * ya + jax.nn.sigmoid(gb_ref[...]) * yb
    x = x_ref[...] + jnp.dot(mixed.astype(BF16), wout_ref[...], preferred_element_type=F32)
    h = (x * lax.rsqrt(jnp.mean(x * x, axis=-1, keepdims=True) + EPS) * g_ref[...]).astype(BF16)
    acc = x
    for c in range(0, FFN_HIDDEN, FFN_CHUNK):
        gate = jnp.dot(h, wg_ref[:, c:c + FFN_CHUNK], preferred_element_type=F32)
        up = jnp.dot(h, wu_ref[:, c:c + FFN_CHUNK], preferred_element_type=F32)
        u = (jax.nn.silu(gate) * up).astype(BF16)
        acc = acc + jnp.dot(u, wd_ref[c:c + FFN_CHUNK, :], preferred_element_type=F32)
    o_ref[...] = acc


def _mix_ffn(x, oa, ob, ga, gb, wpa, wpb, wout, g, wg, wu, wd):
    t = x.shape[0]
    tm = TM_FFN
    row = lambda i: (i, 0)
    weights = (wpa, wpb, wout, wg, wu, wd)
    blocks = (4 * _nbytes((tm, D_MODEL), F32) + 4 * _nbytes((tm, A_Q_W), BF16)
              + 4 * _nbytes((tm, D_MODEL), F32) + sum(_nbytes(w.shape, BF16) for w in weights)
              + _nbytes((1, D_MODEL), F32))
    temps = 6 * _nbytes((tm, D_MODEL), F32) + 4 * _nbytes((tm, FFN_CHUNK), F32)
    return pl.pallas_call(
        _mix_ffn_kernel,
        grid=(t // tm,),
        in_specs=[
            pl.BlockSpec((tm, D_MODEL), row),
            pl.BlockSpec((tm, A_Q_W), row),
            pl.BlockSpec((tm, B_W), row),
            pl.BlockSpec((tm, D_MODEL), row),
            pl.BlockSpec((tm, D_MODEL), row),
            _const_spec(wpa.shape), _const_spec(wpb.shape), _const_spec(wout.shape),
            _const_spec((1, D_MODEL)),
            _const_spec(wg.shape), _const_spec(wu.shape), _const_spec(wd.shape),
        ],
        out_specs=pl.BlockSpec((tm, D_MODEL), row),
        out_shape=jax.ShapeDtypeStruct((t, D_MODEL), F32),
        compiler_params=pltpu.CompilerParams(
            dimension_semantics=("arbitrary",),
            vmem_limit_bytes=_vmem_limit(blocks, temps)),
        name="mix_ffn",
    )(x, oa, ob, ga, gb, wpa, wpb, wout, g, wg, wu, wd)


def _a_head_order():
    return [kvh * A_GROUP + j for j in range(A_GROUP) for kvh in range(A_KV_HEADS)]


def _rope_tables(positions):
    half = HEAD_DIM // 2
    inv = ROPE_THETA ** (-jnp.arange(half, dtype=F32) / half)
    ang = positions.astype(F32)[:, None] * inv[None, :]
    cos, sin = jnp.cos(ang), jnp.sin(ang)
    reps = LANES // HEAD_DIM
    cos_t = jnp.tile(jnp.concatenate([cos, cos], axis=1), (1, reps))
    sin_t = jnp.tile(jnp.concatenate([-sin, sin], axis=1), (1, reps))
    return cos_t, sin_t


def _nbr_bias_table(rel_bias):
    qc = np.arange(GRID_W)[:, None]
    kc = np.arange(GRID_W)[None, :]
    cs = np.clip(qc - B_WIN_C // 2, 0, GRID_W - B_WIN_C)
    valid = (kc >= cs) & (kc < cs + B_WIN_C)
    dc = np.clip(kc - qc + B_WIN_C - 1, 0, 2 * B_WIN_C - 2)
    full = jnp.where(valid[None, None], rel_bias.astype(F32)[:, :, dc], NEG)
    return jnp.concatenate([full[:, :-1], full[:, 1:]], axis=-1)


def _block_diag_ones():
    i = np.arange(MXU_DIM)
    return jnp.asarray((i[:, None] // HEAD_DIM) == (i[None, :] // HEAD_DIM), BF16)


def kernel(x, positions, attn_norm_g, w_in, q_norm_a, k_norm_a, sink_a, q_norm_b, k_norm_b,
           rel_bias_b, w_proj_a, w_proj_b, w_out, ffn_norm_g, w_gate, w_up, w_down):
    b, s, d = x.shape
    depth = w_in.shape[0]
    t = b * s
    order = _a_head_order()
    col_perm = np.concatenate([np.arange(h * HEAD_DIM, (h + 1) * HEAD_DIM) for h in order])
    in_perm = np.concatenate([col_perm, np.arange(A_Q_W, IN_COLS)])
    cos_t, sin_t = _rope_tables(positions)
    bd = _block_diag_ones()
    scale = HEAD_DIM ** -0.5

    xt = x.reshape(t, d)
    for l in range(depth):
        w_l = w_in[l][:, in_perm].astype(BF16)
        gqa = jnp.tile(q_norm_a[l].astype(F32) * scale, A_HEADS)[None, :]
        gka = jnp.tile(k_norm_a[l].astype(F32), A_KV_HEADS)[None, :]
        gqb = jnp.tile(q_norm_b[l].astype(F32) * scale, B_HEADS)[None, :]
        gkb = jnp.tile(k_norm_b[l].astype(F32), B_HEADS)[None, :]
        qa, ka, va, qb, kb, vb, ga, gb = _in_proj(
            xt, attn_norm_g[l][None, :], w_l, cos_t, sin_t, gqa, gka, gqb, gkb, bd, s)
        oa = _win_attn(sink_a[l][np.asarray(order)].astype(F32),
                       qa.reshape(b, s, A_Q_W), ka.reshape(b, s, A_KV_W), va.reshape(b, s, A_KV_W))
        ob = _nbr_attn(qb.reshape(b, s, B_W), kb.reshape(b, s, B_W), vb.reshape(b, s, B_W),
                       _nbr_bias_table(rel_bias_b[l]))
        xt = _mix_ffn(
            xt, oa.reshape(t, A_Q_W), ob.reshape(t, B_W), ga, gb,
            w_proj_a[l][col_perm, :].astype(BF16), w_proj_b[l].astype(BF16), w_out[l].astype(BF16),
            ffn_norm_g[l][None, :], w_gate[l].astype(BF16), w_up[l].astype(BF16),
            w_down[l].astype(BF16))
    return xt.reshape(b, s, d)
```

```python
import functools

import numpy as np
import jax
import jax.numpy as jnp
from jax import lax
from jax.experimental import pallas as pl
from jax.experimental.pallas import tpu as pltpu

D_MODEL = 1024
HEAD_DIM = 64
A_HEADS = 8
A_KV_HEADS = 2
A_GROUP = A_HEADS // A_KV_HEADS
A_WINDOW = 128
A_BLOCK = 128
A_BAND = 3 * A_BLOCK
B_HEADS = 8
GRID_W = 64
B_WIN_R = 8
B_WIN_C = 16
FFN_HIDDEN = 2816
ROPE_THETA = 10000.0
EPS = 1e-6
NEG = -1e30

A_Q_W = A_HEADS * HEAD_DIM
A_KV_W = A_KV_HEADS * HEAD_DIM
B_W = B_HEADS * HEAD_DIM
IN_WIDTHS = (A_Q_W, A_KV_W, A_KV_W, B_W, B_W, B_W, D_MODEL, D_MODEL)
IN_OFFS = tuple(int(v) for v in np.cumsum((0,) + IN_WIDTHS))
IN_COLS = IN_OFFS[-1]

LANES = 128
MXU_DIM = 256
VMEM_BYTES_V7X = 64 * 1024 * 1024
HEADS_PER_VREG = LANES // HEAD_DIM
A_KV_DUP_W = A_KV_HEADS * LANES
N_PAIRS_A = A_HEADS // HEADS_PER_VREG
N_PAIRS_B = B_HEADS // HEADS_PER_VREG
N_BIAS_ROWS = 2 * B_WIN_R - 1

TM_IN = 512
TM_FFN = 512
FFN_CHUNK = 704
A_BLOCKS_PER_ITER = 2
B_ROWS_PER_ITER = 2

F32 = jnp.float32
BF16 = jnp.bfloat16


def _const_spec(shape):
    nd = len(shape)
    return pl.BlockSpec(shape, lambda *_: (0,) * nd, pipeline_mode=pl.Buffered(1))


def _layer_spec(stacked, layer):
    tail = stacked.shape[1:]
    zeros = (0,) * len(tail)
    return pl.BlockSpec((None,) + tail, lambda *_: (layer,) + zeros, pipeline_mode=pl.Buffered(1))


def _vmem_limit(block_bytes, temp_bytes):
    need = int(block_bytes + temp_bytes)
    assert need < VMEM_BYTES_V7X, need
    return need


def _nbytes(shape, dtype):
    return int(np.prod(shape)) * jnp.dtype(dtype).itemsize


def _in_proj_kernel(x_ref, g_ref, w_ref, cos_ref, sin_ref, gqa_ref, gka_ref, gqb_ref, gkb_ref,
                    bd_ref, qa_ref, ka_ref, va_ref, qb_ref, kb_ref, vb_ref, ga_ref, gb_ref):
    x = x_ref[...]
    h = (x * lax.rsqrt(jnp.mean(x * x, axis=-1, keepdims=True) + EPS) * g_ref[...]).astype(BF16)
    cos = cos_ref[...]
    sin = sin_ref[...]
    lane = lax.broadcasted_iota(jnp.int32, (x.shape[0], LANES), 1)
    first_half = (lane % HEAD_DIM) < (HEAD_DIM // 2)
    lo_half = lane < HEAD_DIM

    def proj(seg, lo, width):
        off = IN_OFFS[seg] + lo
        return jnp.dot(h, w_ref[:, off:off + width], preferred_element_type=F32)

    def head_norm(z, gain):
        zz = z * z
        hi = zz.astype(BF16)
        lo = (zz - hi.astype(F32)).astype(BF16)
        w = z.shape[1]
        bd = bd_ref[:w, :w]
        ss = (jnp.dot(hi, bd, preferred_element_type=F32)
              + jnp.dot(lo, bd, preferred_element_type=F32))
        return z * lax.rsqrt(ss * (1.0 / HEAD_DIM) + EPS) * gain

    def rope(y):
        rot = jnp.where(first_half,
                        pltpu.roll(y, LANES - HEAD_DIM // 2, 1),
                        pltpu.roll(y, HEAD_DIM // 2, 1))
        return y * cos + rot * sin

    def dup_halves(y, out_ref):
        sw = pltpu.roll(y, HEAD_DIM, 1)
        out_ref[:, :LANES] = jnp.where(lo_half, y, sw).astype(out_ref.dtype)
        out_ref[:, LANES:] = jnp.where(lo_half, sw, y).astype(out_ref.dtype)

    def normed(seg, gain_ref, out_ref, with_rope):
        width = IN_WIDTHS[seg]
        step = min(width, MXU_DIM)
        for lo in range(0, width, step):
            y = head_norm(proj(seg, lo, step), gain_ref[:, lo:lo + step])
            for c in range(0, step, LANES):
                piece = y[:, c:c + LANES]
                if with_rope:
                    piece = rope(piece)
                if out_ref is ka_ref:
                    dup_halves(piece, out_ref)
                else:
                    out_ref[:, lo + c:lo + c + LANES] = piece.astype(out_ref.dtype)

    def plain(seg, out_ref):
        width = IN_WIDTHS[seg]
        step = min(width, 2 * MXU_DIM)
        for lo in range(0, width, step):
            out_ref[:, lo:lo + step] = proj(seg, lo, step).astype(out_ref.dtype)

    normed(0, gqa_ref, qa_ref, True)
    normed(1, gka_ref, ka_ref, True)
    dup_halves(proj(2, 0, A_KV_W), va_ref)
    normed(3, gqb_ref, qb_ref, False)
    normed(4, gkb_ref, kb_ref, False)
    plain(5, vb_ref)
    plain(6, ga_ref)
    plain(7, gb_ref)


def _in_proj(x, g_all, w_all, layer, cos, sin, gqa, gka, gqb, gkb, bd, seq):
    t = x.shape[0]
    tm = TM_IN
    tiles_per_seq = seq // tm
    row = lambda i: (i, 0)
    pos = lambda i: (i % tiles_per_seq, 0)
    out_w = (A_Q_W, A_KV_DUP_W, A_KV_DUP_W, B_W, B_W, B_W, D_MODEL, D_MODEL)
    out_dt = (BF16,) * 6 + (F32, F32)
    blocks = (2 * _nbytes((tm, D_MODEL), F32) + _nbytes(w_all.shape[1:], BF16)
              + 4 * _nbytes((tm, LANES), F32)
              + sum(2 * _nbytes((tm, wd), dt) for wd, dt in zip(out_w, out_dt))
              + _nbytes(bd.shape, BF16) + 8 * _nbytes((1, D_MODEL), F32))
    temps = _nbytes((tm, D_MODEL), BF16) + 8 * _nbytes((tm, 2 * MXU_DIM), F32)
    return pl.pallas_call(
        _in_proj_kernel,
        grid=(t // tm,),
        in_specs=[
            pl.BlockSpec((tm, D_MODEL), row),
            _layer_spec(g_all, layer),
            _layer_spec(w_all, layer),
            pl.BlockSpec((tm, LANES), pos),
            pl.BlockSpec((tm, LANES), pos),
            _const_spec(gqa.shape), _const_spec(gka.shape),
            _const_spec(gqb.shape), _const_spec(gkb.shape),
            _const_spec(bd.shape),
        ],
        out_specs=[pl.BlockSpec((tm, wd), row) for wd in out_w],
        out_shape=[jax.ShapeDtypeStruct((t, wd), dt) for wd, dt in zip(out_w, out_dt)],
        compiler_params=pltpu.CompilerParams(
            dimension_semantics=("arbitrary",),
            vmem_limit_bytes=_vmem_limit(blocks, temps)),
        name="in_proj",
    )(x, g_all, w_all, cos, sin, gqa, gka, gqb, gkb, bd)


def _stack_head_pair(qg, lo_half):
    zero = jnp.zeros_like(qg)
    return jnp.concatenate([jnp.where(lo_half, qg, zero), jnp.where(lo_half, zero, qg)], axis=0)


def _win_attn_kernel(sink_ref, q_ref, k_ref, v_ref, o_ref, *, seq):
    nb = seq // A_BLOCK
    lane = lax.broadcasted_iota(jnp.int32, (A_BLOCK, LANES), 1)
    lo_half = lane < HEAD_DIM
    first_head_rows = lax.broadcasted_iota(jnp.int32, (2 * A_BLOCK, 1), 0) < A_BLOCK
    rows_i = lax.broadcasted_iota(jnp.int32, (A_BLOCK, A_BAND), 0)
    cols_i = lax.broadcasted_iota(jnp.int32, (A_BLOCK, A_BAND), 1)

    def body(it, carry):
        units = []
        for bb in range(A_BLOCKS_PER_ITER):
            n = it * A_BLOCKS_PER_ITER + bb
            q0 = pl.multiple_of(n * A_BLOCK, A_BLOCK)
            start = pl.multiple_of(jnp.clip(n * A_BLOCK - A_BLOCK, 0, seq - A_BAND), A_BLOCK)
            valid = jnp.abs((start + cols_i) - (q0 + rows_i)) <= A_WINDOW
            for g in range(N_PAIRS_A):
                kvh = (HEADS_PER_VREG * g) // A_GROUP
                k = k_ref[0, pl.ds(start, A_BAND), kvh * LANES:(kvh + 1) * LANES]
                qs = _stack_head_pair(q_ref[0, pl.ds(q0, A_BLOCK), g * LANES:(g + 1) * LANES], lo_half)
                s = lax.dot_general(qs, k, (((1,), (1,)), ((), ())), preferred_element_type=F32)
                s = jnp.where(valid[None], s.reshape(HEADS_PER_VREG, A_BLOCK, A_BAND), NEG)
                units.append((q0, start, g, kvh, s.reshape(HEADS_PER_VREG * A_BLOCK, A_BAND)))
        probs = []
        for q0, start, g, kvh, s in units:
            sink = jnp.where(first_head_rows, sink_ref[HEADS_PER_VREG * g],
                             sink_ref[HEADS_PER_VREG * g + 1])
            m = jnp.maximum(jnp.max(s, axis=-1, keepdims=True), sink)
            e = jnp.exp(s - m)
            den = jnp.sum(e, axis=-1, keepdims=True) + jnp.exp(sink - m)
            probs.append((e.astype(BF16), den))
        for (q0, start, g, kvh, _), (p, den) in zip(units, probs):
            v = v_ref[0, pl.ds(start, A_BAND), kvh * LANES:(kvh + 1) * LANES]
            o = jnp.dot(p, v, preferred_element_type=F32) / den
            o_ref[0, pl.ds(q0, A_BLOCK), g * LANES:(g + 1) * LANES] = jnp.where(
                lo_half, o[:A_BLOCK], o[A_BLOCK:]).astype(o_ref.dtype)
        return carry

    lax.fori_loop(0, nb // A_BLOCKS_PER_ITER, body, 0)


def _win_attn(sink, q, k, v):
    b, s, _ = q.shape
    assert (s // A_BLOCK) % A_BLOCKS_PER_ITER == 0
    blocks = 4 * _nbytes((s, A_Q_W), BF16) + 4 * _nbytes((s, A_KV_DUP_W), BF16)
    temps = 6 * A_BLOCKS_PER_ITER * N_PAIRS_A * _nbytes((2 * A_BLOCK, A_BAND), F32)
    per_batch = lambda i: (i, 0, 0)
    return pl.pallas_call(
        functools.partial(_win_attn_kernel, seq=s),
        grid=(b,),
        in_specs=[
            pl.BlockSpec(memory_space=pltpu.SMEM),
            pl.BlockSpec((1, s, A_Q_W), per_batch),
            pl.BlockSpec((1, s, A_KV_DUP_W), per_batch),
            pl.BlockSpec((1, s, A_KV_DUP_W), per_batch),
        ],
        out_specs=pl.BlockSpec((1, s, A_Q_W), per_batch),
        out_shape=jax.ShapeDtypeStruct((b, s, A_Q_W), BF16),
        compiler_params=pltpu.CompilerParams(
            dimension_semantics=("arbitrary",),
            vmem_limit_bytes=_vmem_limit(blocks, temps)),
        name="win_attn",
    )(sink, q, k, v)


def _nbr_attn_kernel(q_ref, k_ref, v_ref, bias_ref, o_ref, *, rows):
    nkeys = B_WIN_R * GRID_W
    lane = lax.broadcasted_iota(jnp.int32, (GRID_W, LANES), 1)
    lo_half = lane < HEAD_DIM

    def body(it, carry):
        units = []
        for rr in range(B_ROWS_PER_ITER):
            r = it * B_ROWS_PER_ITER + rr
            rs = jnp.clip(r - B_WIN_R // 2, 0, rows - B_WIN_R)
            q0 = pl.multiple_of(r * GRID_W, GRID_W)
            start = pl.multiple_of(rs * GRID_W, GRID_W)
            dr0 = rs - r + (B_WIN_R - 1)
            for j in range(N_PAIRS_B):
                kg = k_ref[0, pl.ds(start, nkeys), j * LANES:(j + 1) * LANES]
                qs = _stack_head_pair(q_ref[0, pl.ds(q0, GRID_W), j * LANES:(j + 1) * LANES], lo_half)
                s = lax.dot_general(qs, kg, (((1,), (1,)), ((), ())), preferred_element_type=F32)
                s = jnp.concatenate(
                    [s[:, wp * LANES:(wp + 1) * LANES] + bias_ref[j, dr0 + 2 * wp]
                     for wp in range(B_WIN_R // 2)], axis=1)
                units.append((q0, start, j, s))
        probs = []
        for q0, start, j, s in units:
            m = jnp.max(s, axis=-1, keepdims=True)
            e = jnp.exp(s - m)
            probs.append((e.astype(BF16), jnp.sum(e, axis=-1, keepdims=True)))
        for (q0, start, j, _), (p, den) in zip(units, probs):
            vg = v_ref[0, pl.ds(start, nkeys), j * LANES:(j + 1) * LANES]
            o = jnp.dot(p, vg, preferred_element_type=F32) / den
            o_ref[0, pl.ds(q0, GRID_W), j * LANES:(j + 1) * LANES] = jnp.where(
                lo_half, o[:GRID_W], o[GRID_W:]).astype(o_ref.dtype)
        return carry

    lax.fori_loop(0, rows // B_ROWS_PER_ITER, body, 0)


def _nbr_attn(q, k, v, bias_all, layer):
    b, s, _ = q.shape
    rows = s // GRID_W
    assert rows % B_ROWS_PER_ITER == 0
    blocks = 8 * _nbytes((s, B_W), BF16) + _nbytes(bias_all.shape[1:], F32)
    temps = 6 * B_ROWS_PER_ITER * N_PAIRS_B * _nbytes((2 * GRID_W, B_WIN_R * GRID_W), F32)
    per_batch = lambda i: (i, 0, 0)
    return pl.pallas_call(
        functools.partial(_nbr_attn_kernel, rows=rows),
        grid=(b,),
        in_specs=[
            pl.BlockSpec((1, s, B_W), per_batch),
            pl.BlockSpec((1, s, B_W), per_batch),
            pl.BlockSpec((1, s, B_W), per_batch),
            _layer_spec(bias_all, layer),
        ],
        out_specs=pl.BlockSpec((1, s, B_W), per_batch),
        out_shape=jax.ShapeDtypeStruct((b, s, B_W), BF16),
        compiler_params=pltpu.CompilerParams(
            dimension_semantics=("arbitrary",),
            vmem_limit_bytes=_vmem_limit(blocks, temps)),
        name="nbr_attn",
    )(q, k, v, bias_all)


def _mix_ffn_kernel(x_ref, oa_ref, ob_ref, ga_ref, gb_ref, wpa_ref, wpb_ref, wout_ref, g_ref,
                    wg_ref, wu_ref, wd_ref, o_ref):
    ya = jnp.dot(oa_ref[...], wpa_ref[...], preferred_element_type=F32)
    yb = jnp.dot(ob_ref[...], wpb_ref[...], preferred_element_type=F32)
    mixed = jax.nn.sigmoid(ga_ref[...]) * ya + jax.nn.sigmoid(gb_ref[...]) * yb
    x = x_ref[...] + jnp.dot(mixed.astype(BF16), wout_ref[...], preferred_element_type=F32)
    h = (x * lax.rsqrt(jnp.mean(x * x, axis=-1, keepdims=True) + EPS) * g_ref[...]).astype(BF16)
    acc = x
    for c in range(0, FFN_HIDDEN, FFN_CHUNK):
        gate = jnp.dot(h, wg_ref[:, c:c + FFN_CHUNK], preferred_element_type=F32)
        up = jnp.dot(h, wu_ref[:, c:c + FFN_CHUNK], preferred_element_type=F32)
        u = (jax.nn.silu(gate) * up).astype(BF16)
        acc = acc + jnp.dot(u, wd_ref[c:c + FFN_CHUNK, :], preferred_element_type=F32)
    o_ref[...] = acc


def _mix_ffn(x, oa, ob, ga, gb, wpa, wpb, wout, g, wg, wu, wd, layer):
    t = x.shape[0]
    tm = TM_FFN
    row = lambda i: (i, 0)
    weights = (wpa, wpb, wout, wg, wu, wd)
    blocks = (4 * _nbytes((tm, D_MODEL), F32) + 4 * _nbytes((tm, A_Q_W), BF16)
              + 4 * _nbytes((tm, D_MODEL), F32) + sum(_nbytes(w.shape[1:], BF16) for w in weights)
              + _nbytes((1, D_MODEL), F32))
    temps = 6 * _nbytes((tm, D_MODEL), F32) + 4 * _nbytes((tm, FFN_CHUNK), F32)
    return pl.pallas_call(
        _mix_ffn_kernel,
        grid=(t // tm,),
        in_specs=[
            pl.BlockSpec((tm, D_MODEL), row),
            pl.BlockSpec((tm, A_Q_W), row),
            pl.BlockSpec((tm, B_W), row),
            pl.BlockSpec((tm, D_MODEL), row),
            pl.BlockSpec((tm, D_MODEL), row),
            _layer_spec(wpa, layer), _layer_spec(wpb, layer), _layer_spec(wout, layer),
            _layer_spec(g, layer),
            _layer_spec(wg, layer), _layer_spec(wu, layer), _layer_spec(wd, layer),
        ],
        out_specs=pl.BlockSpec((tm, D_MODEL), row),
        out_shape=jax.ShapeDtypeStruct((t, D_MODEL), F32),
        compiler_params=pltpu.CompilerParams(
            dimension_semantics=("arbitrary",),
            vmem_limit_bytes=_vmem_limit(blocks, temps)),
        name="mix_ffn",
    )(x, oa, ob, ga, gb, wpa, wpb, wout, g, wg, wu, wd)


def _rope_tables(positions):
    half = HEAD_DIM // 2
    inv = ROPE_THETA ** (-jnp.arange(half, dtype=F32) / half)
    ang = positions.astype(F32)[:, None] * inv[None, :]
    cos, sin = jnp.cos(ang), jnp.sin(ang)
    cos_t = jnp.tile(jnp.concatenate([cos, cos], axis=1), (1, HEADS_PER_VREG))
    sin_t = jnp.tile(jnp.concatenate([-sin, sin], axis=1), (1, HEADS_PER_VREG))
    return cos_t, sin_t


def _nbr_bias_tables(rel_bias):
    depth = rel_bias.shape[0]
    qc = np.arange(GRID_W)[:, None]
    kc = np.arange(GRID_W)[None, :]
    cs = np.clip(qc - B_WIN_C // 2, 0, GRID_W - B_WIN_C)
    valid = (kc >= cs) & (kc < cs + B_WIN_C)
    dc = kc - qc + B_WIN_C - 1
    onehot = (dc[None] == np.arange(2 * B_WIN_C - 1)[:, None, None]) & valid[None]
    full = jnp.einsum('lhdc,cqk->lhdqk', rel_bias.astype(F32), jnp.asarray(onehot, F32),
                      precision=lax.Precision.HIGHEST)
    full = full + jnp.asarray(np.where(valid, 0.0, NEG), F32)
    pairs = jnp.concatenate([full[:, :, :-1], full[:, :, 1:]], axis=-1)
    pairs = pairs.reshape(depth, N_PAIRS_B, HEADS_PER_VREG, N_BIAS_ROWS - 1, GRID_W, LANES)
    pairs = pairs.transpose(0, 1, 3, 2, 4, 5)
    return pairs.reshape(depth, N_PAIRS_B, N_BIAS_ROWS - 1, HEADS_PER_VREG * GRID_W, LANES)


def _block_diag_ones():
    i = np.arange(MXU_DIM)
    return jnp.asarray((i[:, None] // HEAD_DIM) == (i[None, :] // HEAD_DIM), BF16)


def kernel(x, positions, attn_norm_g, w_in, q_norm_a, k_norm_a, sink_a, q_norm_b, k_norm_b,
           rel_bias_b, w_proj_a, w_proj_b, w_out, ffn_norm_g, w_gate, w_up, w_down):
    b, s, d = x.shape
    depth = w_in.shape[0]
    t = b * s
    cos_t, sin_t = _rope_tables(positions)
    bd = _block_diag_ones()
    bias_all = _nbr_bias_tables(rel_bias_b)
    scale = HEAD_DIM ** -0.5
    w_in_h, w_pa_h, w_pb_h, w_out_h, w_gate_h, w_up_h, w_down_h = (
        w.astype(BF16) for w in (w_in, w_proj_a, w_proj_b, w_out, w_gate, w_up, w_down))
    attn_g = attn_norm_g.astype(F32)[:, None, :]
    ffn_g = ffn_norm_g.astype(F32)[:, None, :]
    sink_all = sink_a.astype(F32)

    xt = x.reshape(t, d)
    for l in range(depth):
        gqa = jnp.tile(q_norm_a[l].astype(F32) * scale, A_HEADS)[None, :]
        gka = jnp.tile(k_norm_a[l].astype(F32), A_KV_HEADS)[None, :]
        gqb = jnp.tile(q_norm_b[l].astype(F32) * scale, B_HEADS)[None, :]
        gkb = jnp.tile(k_norm_b[l].astype(F32), B_HEADS)[None, :]
        qa, ka, va, qb, kb, vb, ga, gb = _in_proj(
            xt, attn_g, w_in_h, l, cos_t, sin_t, gqa, gka, gqb, gkb, bd, s)
        oa = _win_attn(sink_all[l], qa.reshape(b, s, A_Q_W), ka.reshape(b, s, A_KV_DUP_W),
                       va.reshape(b, s, A_KV_DUP_W))
        ob = _nbr_attn(qb.reshape(b, s, B_W), kb.reshape(b, s, B_W), vb.reshape(b, s, B_W),
                       bias_all, l)
        xt = _mix_ffn(xt, oa.reshape(t, A_Q_W), ob.reshape(t, B_W), ga, gb,
                      w_pa_h, w_pb_h, w_out_h, ffn_g, w_gate_h, w_up_h, w_down_h, l)
    return xt.reshape(b, s, d)
```

```python
import functools

import numpy as np
import jax
import jax.numpy as jnp
from jax import lax
from jax.experimental import pallas as pl
from jax.experimental.pallas import tpu as pltpu

D_MODEL = 1024
HEAD_DIM = 64
A_HEADS = 8
A_KV_HEADS = 2
A_GROUP = A_HEADS // A_KV_HEADS
A_WINDOW = 128
A_BLOCK = 128
A_BAND = 3 * A_BLOCK
B_HEADS = 8
GRID_W = 64
B_WIN_R = 8
B_WIN_C = 16
FFN_HIDDEN = 2816
ROPE_THETA = 10000.0
EPS = 1e-6
NEG = -1e30

A_Q_W = A_HEADS * HEAD_DIM
A_KV_W = A_KV_HEADS * HEAD_DIM
B_W = B_HEADS * HEAD_DIM
IN_WIDTHS = (A_Q_W, A_KV_W, A_KV_W, B_W, B_W, B_W, D_MODEL, D_MODEL)
IN_OFFS = tuple(int(v) for v in np.cumsum((0,) + IN_WIDTHS))
IN_COLS = IN_OFFS[-1]

LANES = 128
MXU_DIM = 256
VMEM_BYTES_V7X = 64 * 1024 * 1024
HEADS_PER_VREG = LANES // HEAD_DIM
A_KV_DUP_W = A_KV_HEADS * LANES
N_PAIRS_A = A_HEADS // HEADS_PER_VREG
N_PAIRS_B = B_HEADS // HEADS_PER_VREG
N_BIAS_ROWS = 2 * B_WIN_R - 1

TM_IN = 512
TM_FFN = 512
FFN_CHUNK = 3 * MXU_DIM
A_BLOCKS_PER_ITER = 2
B_ROWS_PER_ITER = 2

F32 = jnp.float32
BF16 = jnp.bfloat16


def _const_spec(shape):
    nd = len(shape)
    return pl.BlockSpec(shape, lambda *_: (0,) * nd, pipeline_mode=pl.Buffered(1))


def _layer_spec(stacked, layer):
    tail = stacked.shape[1:]
    zeros = (0,) * len(tail)
    return pl.BlockSpec((None,) + tail, lambda *_: (layer,) + zeros, pipeline_mode=pl.Buffered(1))


def _vmem_limit(block_bytes, temp_bytes):
    need = int(block_bytes + temp_bytes)
    assert need < VMEM_BYTES_V7X, need
    return need


def _nbytes(shape, dtype):
    return int(np.prod(shape)) * jnp.dtype(dtype).itemsize


def _in_proj_kernel(x_ref, g_ref, w_ref, cos_ref, sin_ref, gqa_ref, gka_ref, gqb_ref, gkb_ref,
                    bd_ref, qa_ref, ka_ref, va_ref, qb_ref, kb_ref, vb_ref, ga_ref, gb_ref):
    x = x_ref[...]
    h = (x * lax.rsqrt(jnp.mean(x * x, axis=-1, keepdims=True) + EPS) * g_ref[...]).astype(BF16)
    cos = cos_ref[...]
    sin = sin_ref[...]
    lane = lax.broadcasted_iota(jnp.int32, (x.shape[0], LANES), 1)
    first_half = (lane % HEAD_DIM) < (HEAD_DIM // 2)
    lo_half = lane < HEAD_DIM

    def proj(seg, lo, width):
        off = IN_OFFS[seg] + lo
        return jnp.dot(h, w_ref[:, off:off + width], preferred_element_type=F32)

    def sum_squares(z):
        zz = z * z
        hi = zz.astype(BF16)
        lo = (zz - hi.astype(F32)).astype(BF16)
        w = z.shape[1]
        bd = bd_ref[:w, :w]
        return (jnp.dot(hi, bd, preferred_element_type=F32)
                + jnp.dot(lo, bd, preferred_element_type=F32))

    def rope(y):
        rot = jnp.where(first_half,
                        pltpu.roll(y, LANES - HEAD_DIM // 2, 1),
                        pltpu.roll(y, HEAD_DIM // 2, 1))
        return y * cos + rot * sin

    def dup_halves(y, out_ref):
        sw = pltpu.roll(y, HEAD_DIM, 1)
        out_ref[:, :LANES] = jnp.where(lo_half, y, sw).astype(out_ref.dtype)
        out_ref[:, LANES:] = jnp.where(lo_half, sw, y).astype(out_ref.dtype)

    def finish(z, ss, gain_ref, out_ref, lo, with_rope):
        y = z * lax.rsqrt(ss * (1.0 / HEAD_DIM) + EPS) * gain_ref[:, lo:lo + z.shape[1]]
        for c in range(0, z.shape[1], LANES):
            piece = y[:, c:c + LANES]
            if with_rope:
                piece = rope(piece)
            if out_ref is ka_ref:
                dup_halves(piece, out_ref)
            else:
                out_ref[:, lo + c:lo + c + LANES] = piece.astype(out_ref.dtype)

    normed = [(0, gqa_ref, qa_ref, True), (1, gka_ref, ka_ref, True),
              (3, gqb_ref, qb_ref, False), (4, gkb_ref, kb_ref, False)]
    pieces = [(seg, gain, out, rp, lo, min(IN_WIDTHS[seg], MXU_DIM))
              for seg, gain, out, rp in normed
              for lo in range(0, IN_WIDTHS[seg], min(IN_WIDTHS[seg], MXU_DIM))]
    plain = [(5, vb_ref, 0), (6, ga_ref, 0), (6, ga_ref, 2 * MXU_DIM),
             (7, gb_ref, 0), (7, gb_ref, 2 * MXU_DIM)]

    zs = [proj(seg, lo, width) for seg, _, _, _, lo, width in pieces]
    dup_halves(proj(2, 0, A_KV_W), va_ref)
    sss = []
    for i, z in enumerate(zs):
        sss.append(sum_squares(z))
        if i < len(plain):
            seg, out_ref, lo = plain[i]
            out_ref[:, lo:lo + 2 * MXU_DIM] = proj(seg, lo, 2 * MXU_DIM).astype(out_ref.dtype)
    for (seg, gain_ref, out_ref, rp, lo, width), z, ss in zip(pieces, zs, sss):
        finish(z, ss, gain_ref, out_ref, lo, rp)


def _in_proj(x, g_all, w_all, layer, cos, sin, gqa, gka, gqb, gkb, bd, seq):
    t = x.shape[0]
    tm = TM_IN
    tiles_per_seq = seq // tm
    row = lambda i: (i, 0)
    pos = lambda i: (i % tiles_per_seq, 0)
    out_w = (A_Q_W, A_KV_DUP_W, A_KV_DUP_W, B_W, B_W, B_W, D_MODEL, D_MODEL)
    out_dt = (BF16,) * 6 + (F32, F32)
    blocks = (2 * _nbytes((tm, D_MODEL), F32) + _nbytes(w_all.shape[1:], BF16)
              + 4 * _nbytes((tm, LANES), F32)
              + sum(2 * _nbytes((tm, wd), dt) for wd, dt in zip(out_w, out_dt))
              + _nbytes(bd.shape, BF16) + 8 * _nbytes((1, D_MODEL), F32))
    normed_w = A_Q_W + A_KV_W + 2 * B_W
    temps = (_nbytes((tm, D_MODEL), BF16) + 3 * _nbytes((tm, normed_w), F32)
             + 4 * _nbytes((tm, 2 * MXU_DIM), F32))
    return pl.pallas_call(
        _in_proj_kernel,
        grid=(t // tm,),
        in_specs=[
            pl.BlockSpec((tm, D_MODEL), row),
            _layer_spec(g_all, layer),
            _layer_spec(w_all, layer),
            pl.BlockSpec((tm, LANES), pos),
            pl.BlockSpec((tm, LANES), pos),
            _const_spec(gqa.shape), _const_spec(gka.shape),
            _const_spec(gqb.shape), _const_spec(gkb.shape),
            _const_spec(bd.shape),
        ],
        out_specs=[pl.BlockSpec((tm, wd), row) for wd in out_w],
        out_shape=[jax.ShapeDtypeStruct((t, wd), dt) for wd, dt in zip(out_w, out_dt)],
        compiler_params=pltpu.CompilerParams(
            dimension_semantics=("arbitrary",),
            vmem_limit_bytes=_vmem_limit(blocks, temps)),
        name="in_proj",
    )(x, g_all, w_all, cos, sin, gqa, gka, gqb, gkb, bd)


def _stack_head_pair(qg, lo_half):
    zero = jnp.zeros_like(qg)
    return jnp.concatenate([jnp.where(lo_half, qg, zero), jnp.where(lo_half, zero, qg)], axis=0)


def _win_attn_kernel(sink_ref, q_ref, k_ref, v_ref, o_ref, *, seq):
    nb = seq // A_BLOCK
    lane = lax.broadcasted_iota(jnp.int32, (A_BLOCK, LANES), 1)
    lo_half = lane < HEAD_DIM
    first_head_rows = lax.broadcasted_iota(jnp.int32, (2 * A_BLOCK, 1), 0) < A_BLOCK
    rows_i = lax.broadcasted_iota(jnp.int32, (A_BLOCK, A_BAND), 0)
    cols_i = lax.broadcasted_iota(jnp.int32, (A_BLOCK, A_BAND), 1)

    def body(it, carry):
        units = []
        for bb in range(A_BLOCKS_PER_ITER):
            n = it * A_BLOCKS_PER_ITER + bb
            q0 = pl.multiple_of(n * A_BLOCK, A_BLOCK)
            start = pl.multiple_of(jnp.clip(n * A_BLOCK - A_BLOCK, 0, seq - A_BAND), A_BLOCK)
            valid = jnp.abs((start + cols_i) - (q0 + rows_i)) <= A_WINDOW
            for g in range(N_PAIRS_A):
                kvh = (HEADS_PER_VREG * g) // A_GROUP
                k = k_ref[0, pl.ds(start, A_BAND), kvh * LANES:(kvh + 1) * LANES]
                qs = _stack_head_pair(q_ref[0, pl.ds(q0, A_BLOCK), g * LANES:(g + 1) * LANES], lo_half)
                s = lax.dot_general(qs, k, (((1,), (1,)), ((), ())), preferred_element_type=F32)
                s = jnp.where(valid[None], s.reshape(HEADS_PER_VREG, A_BLOCK, A_BAND), NEG)
                units.append((q0, start, g, kvh, s.reshape(HEADS_PER_VREG * A_BLOCK, A_BAND)))
        probs = []
        for q0, start, g, kvh, s in units:
            sink = jnp.where(first_head_rows, sink_ref[HEADS_PER_VREG * g],
                             sink_ref[HEADS_PER_VREG * g + 1])
            m = jnp.maximum(jnp.max(s, axis=-1, keepdims=True), sink)
            e = jnp.exp(s - m)
            den = jnp.sum(e, axis=-1, keepdims=True) + jnp.exp(sink - m)
            probs.append((e.astype(BF16), den))
        for (q0, start, g, kvh, _), (p, den) in zip(units, probs):
            v = v_ref[0, pl.ds(start, A_BAND), kvh * LANES:(kvh + 1) * LANES]
            o = jnp.dot(p, v, preferred_element_type=F32) / den
            o_ref[0, pl.ds(q0, A_BLOCK), g * LANES:(g + 1) * LANES] = jnp.where(
                lo_half, o[:A_BLOCK], o[A_BLOCK:]).astype(o_ref.dtype)
        return carry

    lax.fori_loop(0, nb // A_BLOCKS_PER_ITER, body, 0)


def _win_attn(sink, q, k, v):
    b, s, _ = q.shape
    assert (s // A_BLOCK) % A_BLOCKS_PER_ITER == 0
    blocks = 4 * _nbytes((s, A_Q_W), BF16) + 4 * _nbytes((s, A_KV_DUP_W), BF16)
    temps = 6 * A_BLOCKS_PER_ITER * N_PAIRS_A * _nbytes((2 * A_BLOCK, A_BAND), F32)
    per_batch = lambda i: (i, 0, 0)
    return pl.pallas_call(
        functools.partial(_win_attn_kernel, seq=s),
        grid=(b,),
        in_specs=[
            pl.BlockSpec(memory_space=pltpu.SMEM),
            pl.BlockSpec((1, s, A_Q_W), per_batch),
            pl.BlockSpec((1, s, A_KV_DUP_W), per_batch),
            pl.BlockSpec((1, s, A_KV_DUP_W), per_batch),
        ],
        out_specs=pl.BlockSpec((1, s, A_Q_W), per_batch),
        out_shape=jax.ShapeDtypeStruct((b, s, A_Q_W), BF16),
        compiler_params=pltpu.CompilerParams(
            dimension_semantics=("arbitrary",),
            vmem_limit_bytes=_vmem_limit(blocks, temps)),
        name="win_attn",
    )(sink, q, k, v)


def _nbr_attn_kernel(q_ref, k_ref, v_ref, bias_ref, o_ref, *, rows):
    nkeys = B_WIN_R * GRID_W
    lane = lax.broadcasted_iota(jnp.int32, (GRID_W, LANES), 1)
    lo_half = lane < HEAD_DIM

    def body(it, carry):
        units = []
        for rr in range(B_ROWS_PER_ITER):
            r = it * B_ROWS_PER_ITER + rr
            rs = jnp.clip(r - B_WIN_R // 2, 0, rows - B_WIN_R)
            q0 = pl.multiple_of(r * GRID_W, GRID_W)
            start = pl.multiple_of(rs * GRID_W, GRID_W)
            dr0 = rs - r + (B_WIN_R - 1)
            for j in range(N_PAIRS_B):
                kg = k_ref[0, pl.ds(start, nkeys), j * LANES:(j + 1) * LANES]
                qs = _stack_head_pair(q_ref[0, pl.ds(q0, GRID_W), j * LANES:(j + 1) * LANES], lo_half)
                s = lax.dot_general(qs, kg, (((1,), (1,)), ((), ())), preferred_element_type=F32)
                s = jnp.concatenate(
                    [s[:, wp * LANES:(wp + 1) * LANES] + bias_ref[j, dr0 + 2 * wp]
                     for wp in range(B_WIN_R // 2)], axis=1)
                units.append((q0, start, j, s))
        probs = []
        for q0, start, j, s in units:
            m = jnp.max(s, axis=-1, keepdims=True)
            e = jnp.exp(s - m)
            probs.append((e.astype(BF16), jnp.sum(e, axis=-1, keepdims=True)))
        for (q0, start, j, _), (p, den) in zip(units, probs):
            vg = v_ref[0, pl.ds(start, nkeys), j * LANES:(j + 1) * LANES]
            o = jnp.dot(p, vg, preferred_element_type=F32) / den
            o_ref[0, pl.ds(q0, GRID_W), j * LANES:(j + 1) * LANES] = jnp.where(
                lo_half, o[:GRID_W], o[GRID_W:]).astype(o_ref.dtype)
        return carry

    lax.fori_loop(0, rows // B_ROWS_PER_ITER, body, 0)


def _nbr_attn(q, k, v, bias_all, layer):
    b, s, _ = q.shape
    rows = s // GRID_W
    assert rows % B_ROWS_PER_ITER == 0
    blocks = 8 * _nbytes((s, B_W), BF16) + _nbytes(bias_all.shape[1:], F32)
    temps = 6 * B_ROWS_PER_ITER * N_PAIRS_B * _nbytes((2 * GRID_W, B_WIN_R * GRID_W), F32)
    per_batch = lambda i: (i, 0, 0)
    return pl.pallas_call(
        functools.partial(_nbr_attn_kernel, rows=rows),
        grid=(b,),
        in_specs=[
            pl.BlockSpec((1, s, B_W), per_batch),
            pl.BlockSpec((1, s, B_W), per_batch),
            pl.BlockSpec((1, s, B_W), per_batch),
            _layer_spec(bias_all, layer),
        ],
        out_specs=pl.BlockSpec((1, s, B_W), per_batch),
        out_shape=jax.ShapeDtypeStruct((b, s, B_W), BF16),
        compiler_params=pltpu.CompilerParams(
            dimension_semantics=("arbitrary",),
            vmem_limit_bytes=_vmem_limit(blocks, temps)),
        name="nbr_attn",
    )(q, k, v, bias_all)


def _mix_ffn_kernel(x_ref, oa_ref, ob_ref, ga_ref, gb_ref, wpa_ref, wpb_ref, wout_ref, g_ref,
                    wg_ref, wu_ref, wd_ref, o_ref):
    ya = jnp.dot(oa_ref[...], wpa_ref[...], preferred_element_type=F32)
    yb = jnp.dot(ob_ref[...], wpb_ref[...], preferred_element_type=F32)
    mixed = jax.nn.sigmoid(ga_ref[...]) * ya + jax.nn.sigmoid(gb_ref[...]) * yb
    x = x_ref[...] + jnp.dot(mixed.astype(BF16), wout_ref[...], preferred_element_type=F32)
    h = (x * lax.rsqrt(jnp.mean(x * x, axis=-1, keepdims=True) + EPS) * g_ref[...]).astype(BF16)
    acc = x
    for c in range(0, FFN_HIDDEN, FFN_CHUNK):
        e = min(c + FFN_CHUNK, FFN_HIDDEN)
        gate = jnp.dot(h, wg_ref[:, c:e], preferred_element_type=F32)
        up = jnp.dot(h, wu_ref[:, c:e], preferred_element_type=F32)
        u = (jax.nn.silu(gate) * up).astype(BF16)
        acc = acc + jnp.dot(u, wd_ref[c:e, :], preferred_element_type=F32)
    o_ref[...] = acc


def _mix_ffn(x, oa, ob, ga, gb, wpa, wpb, wout, g, wg, wu, wd, layer):
    t = x.shape[0]
    tm = TM_FFN
    row = lambda i: (i, 0)
    weights = (wpa, wpb, wout, wg, wu, wd)
    blocks = (4 * _nbytes((tm, D_MODEL), F32) + 4 * _nbytes((tm, A_Q_W), BF16)
              + 4 * _nbytes((tm, D_MODEL), F32) + sum(_nbytes(w.shape[1:], BF16) for w in weights)
              + _nbytes((1, D_MODEL), F32))
    temps = 6 * _nbytes((tm, D_MODEL), F32) + 4 * _nbytes((tm, FFN_CHUNK), F32)
    return pl.pallas_call(
        _mix_ffn_kernel,
        grid=(t // tm,),
        in_specs=[
            pl.BlockSpec((tm, D_MODEL), row),
            pl.BlockSpec((tm, A_Q_W), row),
            pl.BlockSpec((tm, B_W), row),
            pl.BlockSpec((tm, D_MODEL), row),
            pl.BlockSpec((tm, D_MODEL), row),
            _layer_spec(wpa, layer), _layer_spec(wpb, layer), _layer_spec(wout, layer),
            _layer_spec(g, layer),
            _layer_spec(wg, layer), _layer_spec(wu, layer), _layer_spec(wd, layer),
        ],
        out_specs=pl.BlockSpec((tm, D_MODEL), row),
        out_shape=jax.ShapeDtypeStruct((t, D_MODEL), F32),
        compiler_params=pltpu.CompilerParams(
            dimension_semantics=("arbitrary",),
            vmem_limit_bytes=_vmem_limit(blocks, temps)),
        name="mix_ffn",
    )(x, oa, ob, ga, gb, wpa, wpb, wout, g, wg, wu, wd)


def _rope_tables(positions):
    half = HEAD_DIM // 2
    inv = ROPE_THETA ** (-jnp.arange(half, dtype=F32) / half)
    ang = positions.astype(F32)[:, None] * inv[None, :]
    cos, sin = jnp.cos(ang), jnp.sin(ang)
    cos_t = jnp.tile(jnp.concatenate([cos, cos], axis=1), (1, HEADS_PER_VREG))
    sin_t = jnp.tile(jnp.concatenate([-sin, sin], axis=1), (1, HEADS_PER_VREG))
    return cos_t, sin_t


def _nbr_bias_tables(rel_bias):
    depth = rel_bias.shape[0]
    qc = np.arange(GRID_W)[:, None]
    kc = np.arange(GRID_W)[None, :]
    cs = np.clip(qc - B_WIN_C // 2, 0, GRID_W - B_WIN_C)
    valid = (kc >= cs) & (kc < cs + B_WIN_C)
    dc = kc - qc + B_WIN_C - 1
    onehot = (dc[None] == np.arange(2 * B_WIN_C - 1)[:, None, None]) & valid[None]
    full = jnp.einsum('lhdc,cqk->lhdqk', rel_bias.astype(F32), jnp.asarray(onehot, F32),
                      precision=lax.Precision.HIGHEST)
    full = full + jnp.asarray(np.where(valid, 0.0, NEG), F32)
    pairs = jnp.concatenate([full[:, :, :-1], full[:, :, 1:]], axis=-1)
    pairs = pairs.reshape(depth, N_PAIRS_B, HEADS_PER_VREG, N_BIAS_ROWS - 1, GRID_W, LANES)
    pairs = pairs.transpose(0, 1, 3, 2, 4, 5)
    return pairs.reshape(depth, N_PAIRS_B, N_BIAS_ROWS - 1, HEADS_PER_VREG * GRID_W, LANES)


def _block_diag_ones():
    i = np.arange(MXU_DIM)
    return jnp.asarray((i[:, None] // HEAD_DIM) == (i[None, :] // HEAD_DIM), BF16)


def kernel(x, positions, attn_norm_g, w_in, q_norm_a, k_norm_a, sink_a, q_norm_b, k_norm_b,
           rel_bias_b, w_proj_a, w_proj_b, w_out, ffn_norm_g, w_gate, w_up, w_down):
    b, s, d = x.shape
    depth = w_in.shape[0]
    t = b * s
    cos_t, sin_t = _rope_tables(positions)
    bd = _block_diag_ones()
    bias_all = _nbr_bias_tables(rel_bias_b)
    scale = HEAD_DIM ** -0.5
    w_in_h, w_pa_h, w_pb_h, w_out_h, w_gate_h, w_up_h, w_down_h = (
        w.astype(BF16) for w in (w_in, w_proj_a, w_proj_b, w_out, w_gate, w_up, w_down))
    attn_g = attn_norm_g.astype(F32)[:, None, :]
    ffn_g = ffn_norm_g.astype(F32)[:, None, :]
    sink_all = sink_a.astype(F32)

    xt = x.reshape(t, d)
    for l in range(depth):
        gqa = jnp.tile(q_norm_a[l].astype(F32) * scale, A_HEADS)[None, :]
        gka = jnp.tile(k_norm_a[l].astype(F32), A_KV_HEADS)[None, :]
        gqb = jnp.tile(q_norm_b[l].astype(F32) * scale, B_HEADS)[None, :]
        gkb = jnp.tile(k_norm_b[l].astype(F32), B_HEADS)[None, :]
        qa, ka, va, qb, kb, vb, ga, gb = _in_proj(
            xt, attn_g, w_in_h, l, cos_t, sin_t, gqa, gka, gqb, gkb, bd, s)
        oa = _win_attn(sink_all[l], qa.reshape(b, s, A_Q_W), ka.reshape(b, s, A_KV_DUP_W),
                       va.reshape(b, s, A_KV_DUP_W))
        ob = _nbr_attn(qb.reshape(b, s, B_W), kb.reshape(b, s, B_W), vb.reshape(b, s, B_W),
                       bias_all, l)
        xt = _mix_ffn(xt, oa.reshape(t, A_Q_W), ob.reshape(t, B_W), ga, gb,
                      w_pa_h, w_pb_h, w_out_h, ffn_g, w_gate_h, w_up_h, w_down_h, l)
    return xt.reshape(b, s, d)
```

```python
import functools

import numpy as np
import jax
import jax.numpy as jnp
from jax import lax
from jax.experimental import pallas as pl
from jax.experimental.pallas import tpu as pltpu

D_MODEL = 1024
HEAD_DIM = 64
A_HEADS = 8
A_KV_HEADS = 2
A_GROUP = A_HEADS // A_KV_HEADS
A_WINDOW = 128
A_BLOCK = 128
A_BAND = 3 * A_BLOCK
B_HEADS = 8
GRID_W = 64
B_WIN_R = 8
B_WIN_C = 16
FFN_HIDDEN = 2816
ROPE_THETA = 10000.0
EPS = 1e-6
NEG = -1e30
LOG2E = float(np.log2(np.e))

A_Q_W = A_HEADS * HEAD_DIM
A_KV_W = A_KV_HEADS * HEAD_DIM
B_W = B_HEADS * HEAD_DIM
IN_WIDTHS = (A_Q_W, A_KV_W, A_KV_W, B_W, B_W, B_W, D_MODEL, D_MODEL)
IN_OFFS = tuple(int(v) for v in np.cumsum((0,) + IN_WIDTHS))
IN_COLS = IN_OFFS[-1]

LANES = 128
MXU_DIM = 256
VMEM_BYTES_V7X = 64 * 1024 * 1024
HEADS_PER_VREG = LANES // HEAD_DIM
A_KV_DUP_W = A_KV_HEADS * LANES
N_PAIRS_A = A_HEADS // HEADS_PER_VREG
N_PAIRS_B = B_HEADS // HEADS_PER_VREG
N_BIAS_ROWS = 2 * B_WIN_R - 1

TM_IN = 512
TM_FFN = 512
FFN_CHUNK = 3 * MXU_DIM
A_BLOCKS_PER_ITER = 2
B_ROWS_PER_ITER = 2
SOFTMAX_ROWS = 16

F32 = jnp.float32
BF16 = jnp.bfloat16


def _const_spec(shape):
    nd = len(shape)
    return pl.BlockSpec(shape, lambda *_: (0,) * nd, pipeline_mode=pl.Buffered(1))


def _layer_spec(stacked, layer):
    tail = stacked.shape[1:]
    zeros = (0,) * len(tail)
    return pl.BlockSpec((None,) + tail, lambda *_: (layer,) + zeros, pipeline_mode=pl.Buffered(1))


def _vmem_limit(block_bytes, temp_bytes):
    need = int(block_bytes + temp_bytes)
    assert need < VMEM_BYTES_V7X, need
    return need


def _nbytes(shape, dtype):
    return int(np.prod(shape)) * jnp.dtype(dtype).itemsize


def _in_proj_kernel(x_ref, g_ref, w_ref, cos_ref, sin_ref, gqa_ref, gka_ref, gqb_ref, gkb_ref,
                    bd_ref, qa_ref, ka_ref, va_ref, qb_ref, kb_ref, vb_ref, ga_ref, gb_ref):
    x = x_ref[...]
    h = (x * lax.rsqrt(jnp.mean(x * x, axis=-1, keepdims=True) + EPS) * g_ref[...]).astype(BF16)
    cos = cos_ref[...]
    sin = sin_ref[...]
    lane = lax.broadcasted_iota(jnp.int32, (x.shape[0], LANES), 1)
    first_half = (lane % HEAD_DIM) < (HEAD_DIM // 2)
    lo_half = lane < HEAD_DIM

    def proj(seg, lo, width):
        off = IN_OFFS[seg] + lo
        return jnp.dot(h, w_ref[:, off:off + width], preferred_element_type=F32)

    def sum_squares(z):
        zz = z * z
        hi = zz.astype(BF16)
        lo = (zz - hi.astype(F32)).astype(BF16)
        w = z.shape[1]
        bd = bd_ref[:w, :w]
        return (jnp.dot(hi, bd, preferred_element_type=F32)
                + jnp.dot(lo, bd, preferred_element_type=F32))

    def rope(y):
        rot = jnp.where(first_half,
                        pltpu.roll(y, LANES - HEAD_DIM // 2, 1),
                        pltpu.roll(y, HEAD_DIM // 2, 1))
        return y * cos + rot * sin

    def dup_halves(y, out_ref):
        sw = pltpu.roll(y, HEAD_DIM, 1)
        out_ref[:, :LANES] = jnp.where(lo_half, y, sw).astype(out_ref.dtype)
        out_ref[:, LANES:] = jnp.where(lo_half, sw, y).astype(out_ref.dtype)

    def finish(z, ss, gain_ref, out_ref, lo, with_rope):
        y = z * lax.rsqrt(ss * (1.0 / HEAD_DIM) + EPS) * gain_ref[:, lo:lo + z.shape[1]]
        for c in range(0, z.shape[1], LANES):
            piece = y[:, c:c + LANES]
            if with_rope:
                piece = rope(piece)
            if out_ref is ka_ref:
                dup_halves(piece, out_ref)
            else:
                out_ref[:, lo + c:lo + c + LANES] = piece.astype(out_ref.dtype)

    normed = [(0, gqa_ref, qa_ref, True), (1, gka_ref, ka_ref, True),
              (3, gqb_ref, qb_ref, False), (4, gkb_ref, kb_ref, False)]
    pieces = [(seg, gain, out, rp, lo, min(IN_WIDTHS[seg], MXU_DIM))
              for seg, gain, out, rp in normed
              for lo in range(0, IN_WIDTHS[seg], min(IN_WIDTHS[seg], MXU_DIM))]
    plain = [(5, vb_ref, 0), (6, ga_ref, 0), (6, ga_ref, 2 * MXU_DIM),
             (7, gb_ref, 0), (7, gb_ref, 2 * MXU_DIM)]

    zs = [proj(seg, lo, width) for seg, _, _, _, lo, width in pieces]
    dup_halves(proj(2, 0, A_KV_W), va_ref)
    sss = []
    for i, z in enumerate(zs):
        sss.append(sum_squares(z))
        if i < len(plain):
            seg, out_ref, lo = plain[i]
            out_ref[:, lo:lo + 2 * MXU_DIM] = proj(seg, lo, 2 * MXU_DIM).astype(out_ref.dtype)
    for (seg, gain_ref, out_ref, rp, lo, width), z, ss in zip(pieces, zs, sss):
        finish(z, ss, gain_ref, out_ref, lo, rp)


def _in_proj(x, g_all, w_all, layer, cos, sin, gqa, gka, gqb, gkb, bd, seq):
    t = x.shape[0]
    tm = TM_IN
    tiles_per_seq = seq // tm
    row = lambda i: (i, 0)
    pos = lambda i: (i % tiles_per_seq, 0)
    out_w = (A_Q_W, A_KV_DUP_W, A_KV_DUP_W, B_W, B_W, B_W, D_MODEL, D_MODEL)
    out_dt = (BF16,) * 6 + (F32, F32)
    blocks = (2 * _nbytes((tm, D_MODEL), F32) + _nbytes(w_all.shape[1:], BF16)
              + 4 * _nbytes((tm, LANES), F32)
              + sum(2 * _nbytes((tm, wd), dt) for wd, dt in zip(out_w, out_dt))
              + _nbytes(bd.shape, BF16) + 8 * _nbytes((1, D_MODEL), F32))
    normed_w = A_Q_W + A_KV_W + 2 * B_W
    temps = (_nbytes((tm, D_MODEL), BF16) + 3 * _nbytes((tm, normed_w), F32)
             + 4 * _nbytes((tm, 2 * MXU_DIM), F32))
    return pl.pallas_call(
        _in_proj_kernel,
        grid=(t // tm,),
        in_specs=[
            pl.BlockSpec((tm, D_MODEL), row),
            _layer_spec(g_all, layer),
            _layer_spec(w_all, layer),
            pl.BlockSpec((tm, LANES), pos),
            pl.BlockSpec((tm, LANES), pos),
            _const_spec(gqa.shape), _const_spec(gka.shape),
            _const_spec(gqb.shape), _const_spec(gkb.shape),
            _const_spec(bd.shape),
        ],
        out_specs=[pl.BlockSpec((tm, wd), row) for wd in out_w],
        out_shape=[jax.ShapeDtypeStruct((t, wd), dt) for wd, dt in zip(out_w, out_dt)],
        compiler_params=pltpu.CompilerParams(
            dimension_semantics=("arbitrary",),
            vmem_limit_bytes=_vmem_limit(blocks, temps)),
        name="in_proj",
    )(x, g_all, w_all, cos, sin, gqa, gka, gqb, gkb, bd)


def _stack_head_pair(qg, lo_half):
    zero = jnp.zeros_like(qg)
    return jnp.concatenate([jnp.where(lo_half, qg, zero), jnp.where(lo_half, zero, qg)], axis=0)


def _exp2_rows(s_ref, p_ref, u, m):
    nrows, ncols = s_ref.shape[1:]
    parts = []
    for c in range(0, nrows, SOFTMAX_ROWS):
        e = jnp.exp2(s_ref[u, c:c + SOFTMAX_ROWS, :] - m[c:c + SOFTMAX_ROWS])
        p_ref[u, c:c + SOFTMAX_ROWS, :] = e.astype(BF16)
        part = e[:, :LANES]
        for l in range(LANES, ncols, LANES):
            part = part + e[:, l:l + LANES]
        parts.append(part)
    return jnp.sum(jnp.concatenate(parts, axis=0), axis=-1, keepdims=True)


def _win_attn_kernel(sink_ref, q_ref, k_ref, v_ref, mask_ref, o_ref, s_scr, p_scr, *, seq):
    nb = seq // A_BLOCK
    lane = lax.broadcasted_iota(jnp.int32, (A_BLOCK, LANES), 1)
    lo_half = lane < HEAD_DIM
    first_head_rows = lax.broadcasted_iota(jnp.int32, (2 * A_BLOCK, 1), 0) < A_BLOCK

    def body(it, carry):
        units = []
        for bb in range(A_BLOCKS_PER_ITER):
            n = it * A_BLOCKS_PER_ITER + bb
            first = jnp.clip(n - 1, 0, nb - A_BAND // A_BLOCK)
            q0 = pl.multiple_of(n * A_BLOCK, A_BLOCK)
            start = pl.multiple_of(first * A_BLOCK, A_BLOCK)
            mask = mask_ref[n - first]
            for g in range(N_PAIRS_A):
                u = len(units)
                kvh = (HEADS_PER_VREG * g) // A_GROUP
                k = k_ref[0, pl.ds(start, A_BAND), kvh * LANES:(kvh + 1) * LANES]
                qs = _stack_head_pair(q_ref[0, pl.ds(q0, A_BLOCK), g * LANES:(g + 1) * LANES], lo_half)
                s = lax.dot_general(qs, k, (((1,), (1,)), ((), ())), preferred_element_type=F32)
                s = (s.reshape(HEADS_PER_VREG, A_BLOCK, A_BAND) + mask[None]).reshape(s.shape)
                s_scr[u] = s
                sink = jnp.where(first_head_rows, sink_ref[HEADS_PER_VREG * g] * LOG2E,
                                 sink_ref[HEADS_PER_VREG * g + 1] * LOG2E)
                m = jnp.maximum(jnp.max(s, axis=-1, keepdims=True), sink)
                units.append((q0, start, g, kvh, m, sink))
        dens = [_exp2_rows(s_scr, p_scr, u, m) + jnp.exp2(sink - m)
                for u, (_, _, _, _, m, sink) in enumerate(units)]
        for u, (q0, start, g, kvh, _, _) in enumerate(units):
            v = v_ref[0, pl.ds(start, A_BAND), kvh * LANES:(kvh + 1) * LANES]
            o = jnp.dot(p_scr[u], v, preferred_element_type=F32) / dens[u]
            o_ref[0, pl.ds(q0, A_BLOCK), g * LANES:(g + 1) * LANES] = jnp.where(
                lo_half, o[:A_BLOCK], o[A_BLOCK:]).astype(o_ref.dtype)
        return carry

    lax.fori_loop(0, nb // A_BLOCKS_PER_ITER, body, 0)


def _win_mask_table():
    r = np.arange(A_BLOCK)[None, :, None]
    j = np.arange(A_BAND)[None, None, :]
    off = A_BLOCK * np.arange(A_BAND // A_BLOCK)[:, None, None]
    return jnp.asarray(np.where(np.abs(j - off - r) <= A_WINDOW, 0.0, NEG), F32)


def _win_attn(sink, q, k, v):
    b, s, _ = q.shape
    assert (s // A_BLOCK) % A_BLOCKS_PER_ITER == 0
    mask = _win_mask_table()
    unit = (A_BLOCKS_PER_ITER * N_PAIRS_A, HEADS_PER_VREG * A_BLOCK, A_BAND)
    blocks = (4 * _nbytes((s, A_Q_W), BF16) + 4 * _nbytes((s, A_KV_DUP_W), BF16)
              + _nbytes(mask.shape, F32) + _nbytes(unit, F32) + _nbytes(unit, BF16))
    temps = 2 * _nbytes(unit, F32)
    per_batch = lambda i: (i, 0, 0)
    return pl.pallas_call(
        functools.partial(_win_attn_kernel, seq=s),
        grid=(b,),
        in_specs=[
            pl.BlockSpec(memory_space=pltpu.SMEM),
            pl.BlockSpec((1, s, A_Q_W), per_batch),
            pl.BlockSpec((1, s, A_KV_DUP_W), per_batch),
            pl.BlockSpec((1, s, A_KV_DUP_W), per_batch),
            _const_spec(mask.shape),
        ],
        out_specs=pl.BlockSpec((1, s, A_Q_W), per_batch),
        out_shape=jax.ShapeDtypeStruct((b, s, A_Q_W), BF16),
        scratch_shapes=[pltpu.VMEM(unit, F32), pltpu.VMEM(unit, BF16)],
        compiler_params=pltpu.CompilerParams(
            dimension_semantics=("arbitrary",),
            vmem_limit_bytes=_vmem_limit(blocks, temps)),
        name="win_attn",
    )(sink, q, k, v, mask)


def _nbr_attn_kernel(q_ref, k_ref, v_ref, bias_ref, o_ref, s_scr, p_scr, *, rows):
    nkeys = B_WIN_R * GRID_W
    lane = lax.broadcasted_iota(jnp.int32, (GRID_W, LANES), 1)
    lo_half = lane < HEAD_DIM

    def body(it, carry):
        units = []
        for rr in range(B_ROWS_PER_ITER):
            r = it * B_ROWS_PER_ITER + rr
            rs = jnp.clip(r - B_WIN_R // 2, 0, rows - B_WIN_R)
            q0 = pl.multiple_of(r * GRID_W, GRID_W)
            start = pl.multiple_of(rs * GRID_W, GRID_W)
            dr0 = rs - r + (B_WIN_R - 1)
            for j in range(N_PAIRS_B):
                u = len(units)
                kg = k_ref[0, pl.ds(start, nkeys), j * LANES:(j + 1) * LANES]
                qs = _stack_head_pair(q_ref[0, pl.ds(q0, GRID_W), j * LANES:(j + 1) * LANES], lo_half)
                s = lax.dot_general(qs, kg, (((1,), (1,)), ((), ())), preferred_element_type=F32)
                s = jnp.concatenate(
                    [s[:, wp * LANES:(wp + 1) * LANES] + bias_ref[j, dr0 + 2 * wp]
                     for wp in range(B_WIN_R // 2)], axis=1)
                s_scr[u] = s
                units.append((q0, start, j, jnp.max(s, axis=-1, keepdims=True)))
        dens = [_exp2_rows(s_scr, p_scr, u, m) for u, (_, _, _, m) in enumerate(units)]
        for u, (q0, start, j, _) in enumerate(units):
            vg = v_ref[0, pl.ds(start, nkeys), j * LANES:(j + 1) * LANES]
            o = jnp.dot(p_scr[u], vg, preferred_element_type=F32) / dens[u]
            o_ref[0, pl.ds(q0, GRID_W), j * LANES:(j + 1) * LANES] = jnp.where(
                lo_half, o[:GRID_W], o[GRID_W:]).astype(o_ref.dtype)
        return carry

    lax.fori_loop(0, rows // B_ROWS_PER_ITER, body, 0)


def _nbr_attn(q, k, v, bias_all, layer):
    b, s, _ = q.shape
    rows = s // GRID_W
    assert rows % B_ROWS_PER_ITER == 0
    unit = (B_ROWS_PER_ITER * N_PAIRS_B, HEADS_PER_VREG * GRID_W, B_WIN_R * GRID_W)
    blocks = (8 * _nbytes((s, B_W), BF16) + _nbytes(bias_all.shape[1:], F32)
              + _nbytes(unit, F32) + _nbytes(unit, BF16))
    temps = 2 * _nbytes(unit, F32)
    per_batch = lambda i: (i, 0, 0)
    return pl.pallas_call(
        functools.partial(_nbr_attn_kernel, rows=rows),
        grid=(b,),
        in_specs=[
            pl.BlockSpec((1, s, B_W), per_batch),
            pl.BlockSpec((1, s, B_W), per_batch),
            pl.BlockSpec((1, s, B_W), per_batch),
            _layer_spec(bias_all, layer),
        ],
        out_specs=pl.BlockSpec((1, s, B_W), per_batch),
        out_shape=jax.ShapeDtypeStruct((b, s, B_W), BF16),
        scratch_shapes=[pltpu.VMEM(unit, F32), pltpu.VMEM(unit, BF16)],
        compiler_params=pltpu.CompilerParams(
            dimension_semantics=("arbitrary",),
            vmem_limit_bytes=_vmem_limit(blocks, temps)),
        name="nbr_attn",
    )(q, k, v, bias_all)


def _mix_ffn_kernel(x_ref, oa_ref, ob_ref, ga_ref, gb_ref, wpa_ref, wpb_ref, wout_ref, g_ref,
                    wg_ref, wu_ref, wd_ref, o_ref):
    ya = jnp.dot(oa_ref[...], wpa_ref[...], preferred_element_type=F32)
    yb = jnp.dot(ob_ref[...], wpb_ref[...], preferred_element_type=F32)
    mixed = jax.nn.sigmoid(ga_ref[...]) * ya + jax.nn.sigmoid(gb_ref[...]) * yb
    x = x_ref[...] + jnp.dot(mixed.astype(BF16), wout_ref[...], preferred_element_type=F32)
    h = (x * lax.rsqrt(jnp.mean(x * x, axis=-1, keepdims=True) + EPS) * g_ref[...]).astype(BF16)
    acc = x
    for c in range(0, FFN_HIDDEN, FFN_CHUNK):
        e = min(c + FFN_CHUNK, FFN_HIDDEN)
        gate = jnp.dot(h, wg_ref[:, c:e], preferred_element_type=F32)
        up = jnp.dot(h, wu_ref[:, c:e], preferred_element_type=F32)
        u = (jax.nn.silu(gate) * up).astype(BF16)
        acc = acc + jnp.dot(u, wd_ref[c:e, :], preferred_element_type=F32)
    o_ref[...] = acc


def _mix_ffn(x, oa, ob, ga, gb, wpa, wpb, wout, g, wg, wu, wd, layer):
    t = x.shape[0]
    tm = TM_FFN
    row = lambda i: (i, 0)
    weights = (wpa, wpb, wout, wg, wu, wd)
    blocks = (4 * _nbytes((tm, D_MODEL), F32) + 4 * _nbytes((tm, A_Q_W), BF16)
              + 4 * _nbytes((tm, D_MODEL), F32) + sum(_nbytes(w.shape[1:], BF16) for w in weights)
              + _nbytes((1, D_MODEL), F32))
    temps = 6 * _nbytes((tm, D_MODEL), F32) + 4 * _nbytes((tm, FFN_CHUNK), F32)
    return pl.pallas_call(
        _mix_ffn_kernel,
        grid=(t // tm,),
        in_specs=[
            pl.BlockSpec((tm, D_MODEL), row),
            pl.BlockSpec((tm, A_Q_W), row),
            pl.BlockSpec((tm, B_W), row),
            pl.BlockSpec((tm, D_MODEL), row),
            pl.BlockSpec((tm, D_MODEL), row),
            _layer_spec(wpa, layer), _layer_spec(wpb, layer), _layer_spec(wout, layer),
            _layer_spec(g, layer),
            _layer_spec(wg, layer), _layer_spec(wu, layer), _layer_spec(wd, layer),
        ],
        out_specs=pl.BlockSpec((tm, D_MODEL), row),
        out_shape=jax.ShapeDtypeStruct((t, D_MODEL), F32),
        compiler_params=pltpu.CompilerParams(
            dimension_semantics=("arbitrary",),
            vmem_limit_bytes=_vmem_limit(blocks, temps)),
        name="mix_ffn",
    )(x, oa, ob, ga, gb, wpa, wpb, wout, g, wg, wu, wd)


def _rope_tables(positions):
    half = HEAD_DIM // 2
    inv = ROPE_THETA ** (-jnp.arange(half, dtype=F32) / half)
    ang = positions.astype(F32)[:, None] * inv[None, :]
    cos, sin = jnp.cos(ang), jnp.sin(ang)
    cos_t = jnp.tile(jnp.concatenate([cos, cos], axis=1), (1, HEADS_PER_VREG))
    sin_t = jnp.tile(jnp.concatenate([-sin, sin], axis=1), (1, HEADS_PER_VREG))
    return cos_t, sin_t


def _nbr_bias_tables(rel_bias):
    depth = rel_bias.shape[0]
    qc = np.arange(GRID_W)[:, None]
    kc = np.arange(GRID_W)[None, :]
    cs = np.clip(qc - B_WIN_C // 2, 0, GRID_W - B_WIN_C)
    valid = (kc >= cs) & (kc < cs + B_WIN_C)
    dc = kc - qc + B_WIN_C - 1
    onehot = (dc[None] == np.arange(2 * B_WIN_C - 1)[:, None, None]) & valid[None]
    full = jnp.einsum('lhdc,cqk->lhdqk', rel_bias.astype(F32), jnp.asarray(onehot, F32),
                      precision=lax.Precision.HIGHEST)
    full = full * LOG2E + jnp.asarray(np.where(valid, 0.0, NEG), F32)
    pairs = jnp.concatenate([full[:, :, :-1], full[:, :, 1:]], axis=-1)
    pairs = pairs.reshape(depth, N_PAIRS_B, HEADS_PER_VREG, N_BIAS_ROWS - 1, GRID_W, LANES)
    pairs = pairs.transpose(0, 1, 3, 2, 4, 5)
    return pairs.reshape(depth, N_PAIRS_B, N_BIAS_ROWS - 1, HEADS_PER_VREG * GRID_W, LANES)


def _block_diag_ones():
    i = np.arange(MXU_DIM)
    return jnp.asarray((i[:, None] // HEAD_DIM) == (i[None, :] // HEAD_DIM), BF16)


def kernel(x, positions, attn_norm_g, w_in, q_norm_a, k_norm_a, sink_a, q_norm_b, k_norm_b,
           rel_bias_b, w_proj_a, w_proj_b, w_out, ffn_norm_g, w_gate, w_up, w_down):
    b, s, d = x.shape
    depth = w_in.shape[0]
    t = b * s
    cos_t, sin_t = _rope_tables(positions)
    bd = _block_diag_ones()
    bias_all = _nbr_bias_tables(rel_bias_b)
    scale = HEAD_DIM ** -0.5 * LOG2E
    w_in_h, w_pa_h, w_pb_h, w_out_h, w_gate_h, w_up_h, w_down_h = (
        w.astype(BF16) for w in (w_in, w_proj_a, w_proj_b, w_out, w_gate, w_up, w_down))
    attn_g = attn_norm_g.astype(F32)[:, None, :]
    ffn_g = ffn_norm_g.astype(F32)[:, None, :]
    sink_all = sink_a.astype(F32)

    xt = x.reshape(t, d)
    for l in range(depth):
        gqa = jnp.tile(q_norm_a[l].astype(F32) * scale, A_HEADS)[None, :]
        gka = jnp.tile(k_norm_a[l].astype(F32), A_KV_HEADS)[None, :]
        gqb = jnp.tile(q_norm_b[l].astype(F32) * scale, B_HEADS)[None, :]
        gkb = jnp.tile(k_norm_b[l].astype(F32), B_HEADS)[None, :]
        qa, ka, va, qb, kb, vb, ga, gb = _in_proj(
            xt, attn_g, w_in_h, l, cos_t, sin_t, gqa, gka, gqb, gkb, bd, s)
        oa = _win_attn(sink_all[l], qa.reshape(b, s, A_Q_W), ka.reshape(b, s, A_KV_DUP_W),
                       va.reshape(b, s, A_KV_DUP_W))
        ob = _nbr_attn(qb.reshape(b, s, B_W), kb.reshape(b, s, B_W), vb.reshape(b, s, B_W),
                       bias_all, l)
        xt = _mix_ffn(xt, oa.reshape(t, A_Q_W), ob.reshape(t, B_W), ga, gb,
                      w_pa_h, w_pb_h, w_out_h, ffn_g, w_gate_h, w_up_h, w_down_h, l)
    return xt.reshape(b, s, d)
```
